```python
import jax, jax.numpy as jnp
from jax import lax
import numpy as np

D_MODEL = 1024
BATCH = 16
SEQ = 4096
DEPTH = 4
DEC_BATCH = 8
DEC_SEQ = 64
PAST_LEN = 1024

CHUNK = 64
D_MIX = D_MODEL
D_CONV = D_MIX // 2
D_RNN = D_MIX - D_CONV
N_RNN_HEADS = 8
RNN_HEAD_DIM = D_RNN // N_RNN_HEADS
CONV_A_WIDTH = 31
CONV_B_WIDTH = 4
RGLRU_C = 8.0
D_FF = -(-8 * D_MODEL // (3 * 256)) * 256
D_IN = 2 * D_CONV + 2 * D_RNN
RMS_EPS = 1e-6
LN_EPS = 1e-5

kernel_name = "hymba_conformer_rglru_stream_step"


def rmsnorm(x, g):
    xf = x.astype(jnp.float32)
    y = xf * lax.rsqrt(jnp.mean(xf * xf, axis=-1, keepdims=True) + RMS_EPS)
    return (y * g.astype(jnp.float32)).astype(x.dtype)


def layernorm(x, g, b):
    xf = x.astype(jnp.float32)
    mu = jnp.mean(xf, axis=-1, keepdims=True)
    xc = xf - mu
    var = jnp.mean(xc * xc, axis=-1, keepdims=True)
    y = xc * lax.rsqrt(var + LN_EPS) * g.astype(jnp.float32) + b.astype(jnp.float32)
    return y.astype(x.dtype)


def causal_dwconv(x_padded, w, b):
    c = x_padded.shape[-1]
    out = lax.conv_general_dilated(
        x_padded, w[:, None, :].astype(x_padded.dtype), window_strides=(1,), padding='VALID',
        dimension_numbers=('NWC', 'WIO', 'NWC'), feature_group_count=c)
    return out + b


def rg_lru(xc, h0, reset, gate_r_w, gate_r_b, gate_i_w, gate_i_b, lam):
    bsz, s, _ = xc.shape
    xh = xc.reshape(bsz, s, N_RNN_HEADS, RNN_HEAD_DIM)
    r_t = jax.nn.sigmoid(jnp.einsum('bshi,hij->bshj', xh, gate_r_w).reshape(bsz, s, D_RNN) + gate_r_b)
    i_t = jax.nn.sigmoid(jnp.einsum('bshi,hij->bshj', xh, gate_i_w).reshape(bsz, s, D_RNN) + gate_i_b)
    log_a = (-RGLRU_C * r_t.astype(jnp.float32)) * jax.nn.softplus(-lam.astype(jnp.float32))
    a = jnp.exp(log_a)
    mult = jnp.sqrt(jnp.maximum(1.0 - jnp.exp(2.0 * log_a), 0.0))
    rs = reset[None, :, None]
    mult = jnp.where(rs, 1.0, mult)
    a = jnp.where(rs, 0.0, a)
    bterm = mult * (i_t * xc).astype(jnp.float32)
    bterm = bterm.at[:, 0].add(a[:, 0] * h0)

    def combine(left, right):
        a1, b1 = left
        a2, b2 = right
        return a1 * a2, a2 * b1 + b2

    _, h = lax.associative_scan(combine, (a, bterm), axis=1)
    return h.astype(xc.dtype), h[:, -1]


def layer(x, buf_a, buf_b, h0, reset, norm1_g, w_in, conv_a_w, conv_a_b, ln_a_g, ln_a_b,
          conv_b_w, conv_b_b, gate_r_w, gate_r_b, gate_i_w, gate_i_b, lam, w_out,
          norm2_g, w_ffn_in, w_ffn_out):
    hn = rmsnorm(x, norm1_g)
    z = jnp.einsum('bsd,de->bse', hn, w_in)
    a_val, a_gate, b_x, b_gate = jnp.split(
        z, [D_CONV, 2 * D_CONV, 2 * D_CONV + D_RNN], axis=-1)
    u = a_val * jax.nn.sigmoid(a_gate)
    ua = jnp.concatenate([buf_a.astype(u.dtype), u], axis=1)
    new_buf_a = ua[:, -(CONV_A_WIDTH - 1):]
    ca = causal_dwconv(ua, conv_a_w, conv_a_b)
    ya = jax.nn.silu(layernorm(ca, ln_a_g, ln_a_b))
    xb = jnp.concatenate([buf_b.astype(b_x.dtype), b_x], axis=1)
    new_buf_b = xb[:, -(CONV_B_WIDTH - 1):]
    xcb = causal_dwconv(xb, conv_b_w, conv_b_b)
    hseq, h_last = rg_lru(xcb, h0, reset, gate_r_w, gate_r_b, gate_i_w, gate_i_b, lam)
    yb = hseq * jax.nn.gelu(b_gate)
    x = x + jnp.einsum('bse,ed->bsd', jnp.concatenate([ya, yb], axis=-1), w_out)
    hf = jnp.einsum('bsd,df->bsf', rmsnorm(x, norm2_g), w_ffn_in)
    g, v = jnp.split(hf, 2, axis=-1)
    x = x + jnp.einsum('bsf,fd->bsd', jax.nn.silu(g) * v, w_ffn_out)
    return x, new_buf_a, new_buf_b, h_last.astype(x.dtype)


def trunk(x, bufs_a, bufs_b, h0s, first_chunk, norm1_g, w_in, conv_a_w, conv_a_b, ln_a_g,
          ln_a_b, conv_b_w, conv_b_b, gate_r_w, gate_r_b, gate_i_w, gate_i_b, rglru_lambda,
          w_out, norm2_g, w_ffn_in, w_ffn_out, final_norm_g):
    s = x.shape[1]
    if first_chunk:
        reset = jnp.arange(s) == 0
    else:
        reset = jnp.zeros((s,), dtype=bool)
    new_a, new_b, new_h = [], [], []
    for l in range(DEPTH):
        x, na, nb, nh = layer(
            x, bufs_a[l], bufs_b[l], h0s[l].astype(jnp.float32), reset, norm1_g[l], w_in[l],
            conv_a_w[l], conv_a_b[l], ln_a_g[l], ln_a_b[l], conv_b_w[l], conv_b_b[l],
            gate_r_w[l], gate_r_b[l], gate_i_w[l], gate_i_b[l], rglru_lambda[l], w_out[l],
            norm2_g[l], w_ffn_in[l], w_ffn_out[l])
        new_a.append(na)
        new_b.append(nb)
        new_h.append(nh)
    y = rmsnorm(x, final_norm_g)
    return y, jnp.stack(new_a), jnp.stack(new_b), jnp.stack(new_h)


def setup_inputs(seed: int = 0) -> dict:
    key = jax.random.key(seed)
    ks = jax.random.split(key, 24)
    f32 = jnp.float32
    nrm = lambda k, shape, scale: jax.random.normal(k, shape, f32) * scale
    u = jax.random.uniform(ks[16], (DEPTH, D_RNN), f32, 0.9, 0.999)
    s_lam = u ** (1.0 / RGLRU_C)
    rglru_lambda = jnp.log(s_lam) - jnp.log1p(-s_lam)
    return {
        "x_prompt": nrm(ks[0], (BATCH, SEQ, D_MODEL), 1.0),
        "x_sample": nrm(ks[1], (DEC_BATCH, DEC_SEQ, D_MODEL), 1.0),
        "state_conv_a": nrm(ks[2], (DEPTH, DEC_BATCH, CONV_A_WIDTH - 1, D_CONV), 0.5),
        "state_conv_b": nrm(ks[3], (DEPTH, DEC_BATCH, CONV_B_WIDTH - 1, D_RNN), 1.0),
        "state_rglru": nrm(ks[4], (DEPTH, DEC_BATCH, D_RNN), 0.5),
        "norm1_g": 1.0 + nrm(ks[5], (DEPTH, D_MODEL), 0.02),
        "w_in": nrm(ks[6], (DEPTH, D_MODEL, D_IN), D_MODEL ** -0.5),
        "conv_a_w": nrm(ks[7], (DEPTH, CONV_A_WIDTH, D_CONV), CONV_A_WIDTH ** -0.5),
        "conv_a_b": nrm(ks[8], (DEPTH, D_CONV), 0.01),
        "ln_a_g": 1.0 + nrm(ks[9], (DEPTH, D_CONV), 0.02),
        "ln_a_b": nrm(ks[10], (DEPTH, D_CONV), 0.01),
        "conv_b_w": nrm(ks[11], (DEPTH, CONV_B_WIDTH, D_RNN), CONV_B_WIDTH ** -0.5),
        "conv_b_b": nrm(ks[12], (DEPTH, D_RNN), 0.01),
        "gate_r_w": nrm(ks[13], (DEPTH, N_RNN_HEADS, RNN_HEAD_DIM, RNN_HEAD_DIM), RNN_HEAD_DIM ** -0.5),
        "gate_r_b": nrm(ks[14], (DEPTH, D_RNN), 0.01),
        "gate_i_w": nrm(ks[15], (DEPTH, N_RNN_HEADS, RNN_HEAD_DIM, RNN_HEAD_DIM), RNN_HEAD_DIM ** -0.5),
        "gate_i_b": nrm(ks[17], (DEPTH, D_RNN), 0.01),
        "rglru_lambda": rglru_lambda,
        "w_out": nrm(ks[18], (DEPTH, D_MIX, D_MODEL), D_MIX ** -0.5),
        "norm2_g": 1.0 + nrm(ks[19], (DEPTH, D_MODEL), 0.02),
        "w_ffn_in": nrm(ks[20], (DEPTH, D_MODEL, 2 * D_FF), D_MODEL ** -0.5),
        "w_ffn_out": nrm(ks[21], (DEPTH, D_FF, D_MODEL), D_FF ** -0.5),
        "final_norm_g": 1.0 + nrm(ks[22], (D_MODEL,), 0.02),
    }


def reference(x_prompt, x_sample, state_conv_a, state_conv_b, state_rglru, norm1_g, w_in,
              conv_a_w, conv_a_b, ln_a_g, ln_a_b, conv_b_w, conv_b_b, gate_r_w, gate_r_b,
              gate_i_w, gate_i_b, rglru_lambda, w_out, norm2_g, w_ffn_in, w_ffn_out,
              final_norm_g):
    assert x_sample.shape[1] <= CHUNK
    weights = (norm1_g, w_in, conv_a_w, conv_a_b, ln_a_g, ln_a_b, conv_b_w, conv_b_b,
               gate_r_w, gate_r_b, gate_i_w, gate_i_b, rglru_lambda, w_out, norm2_g,
               w_ffn_in, w_ffn_out, final_norm_g)
    b = x_prompt.shape[0]
    zero_a = jnp.zeros((DEPTH, b, CONV_A_WIDTH - 1, D_CONV), x_prompt.dtype)
    zero_b = jnp.zeros((DEPTH, b, CONV_B_WIDTH - 1, D_RNN), x_prompt.dtype)
    zero_h = jnp.zeros((DEPTH, b, D_RNN), x_prompt.dtype)
    y_prompt, pa, pb, ph = trunk(x_prompt, zero_a, zero_b, zero_h, True, *weights)
    y_sample, sa, sb, sh = trunk(x_sample, state_conv_a, state_conv_b, state_rglru, False, *weights)
    return (y_prompt, y_sample, pa, pb, ph, sa, sb, sh)
```

```python
import functools

import jax
import jax.numpy as jnp
from jax import lax
from jax.experimental import pallas as pl
from jax.experimental.pallas import tpu as pltpu

D_MODEL = 1024
D_CONV = 512
D_RNN = 512
N_RNN_HEADS = 8
RNN_HEAD_DIM = D_RNN // N_RNN_HEADS
CONV_A_WIDTH = 31
CONV_B_WIDTH = 4
HIST_A = CONV_A_WIDTH - 1
HIST_B = CONV_B_WIDTH - 1
RGLRU_C = 8.0
D_FF = 2816
RMS_EPS = 1e-6
LN_EPS = 1e-5

MXU_TILE_V7X = 256
GATE_BLOCK = MXU_TILE_V7X
ROWS_PER_TILE = 512
CONV_ROW_BLOCK = 32
FFN_CHUNKS = ((0, 512), (512, 1024), (1024, 1536), (1536, 2048), (2048, 2560), (2560, 2816))
VMEM_LIMIT_BYTES = 56 * 1024 * 1024

_bf16 = jnp.bfloat16
_f32 = jnp.float32


def _dot(a, b):
    return jnp.dot(a, b, preferred_element_type=_f32)


def _rms_scale(x, g):
    return x * lax.rsqrt(jnp.mean(x * x, axis=-1, keepdims=True) + RMS_EPS) * g


def _layer_kernel(x_ref, bufa_ref, bufb_ref, h0_ref, n1g_ref, win_ref, caw_ref, cab_ref,
                  lng_ref, lnb_ref, cbw_ref, cbb_ref, wr_ref, br_ref, wi_ref, bi_ref, lam_ref,
                  wout_ref, n2g_ref, wfi_ref, wfo_ref, fng_ref,
                  xo_ref, na_ref, nb_ref, hl_ref,
                  ua_ref, xb_ref, xc_ref, a_ref, b_ref, gb_ref, ycat_ref, act_ref, h_ref,
                  *, batch, steps, first_chunk, final):
    i = pl.program_id(0)
    rows = batch * steps
    ha = HIST_A * batch
    hb = HIST_B * batch

    @pl.when(i == 0)
    def _load_state():
        ua_ref[0:ha, :] = bufa_ref[...]
        xb_ref[0:hb, :] = bufb_ref[...]
        h_ref[...] = h0_ref[...]

    x = x_ref[...]
    hn = _rms_scale(x, n1g_ref[...]).astype(_bf16)
    a_val = _dot(hn, win_ref[:, 0:D_CONV])
    a_gate = _dot(hn, win_ref[:, D_CONV:2 * D_CONV])
    ua_ref[ha:ha + rows, :] = a_val * jax.nn.sigmoid(a_gate)
    xb_ref[hb:hb + rows, :] = _dot(hn, win_ref[:, 2 * D_CONV:2 * D_CONV + D_RNN])
    gb_ref[...] = jax.nn.gelu(_dot(hn, win_ref[:, 2 * D_CONV + D_RNN:]))

    def conv_a_block(j, carry):
        r0 = pl.multiple_of(j * CONV_ROW_BLOCK, CONV_ROW_BLOCK)
        acc = jnp.broadcast_to(cab_ref[...], (CONV_ROW_BLOCK, D_CONV))
        for k in range(CONV_A_WIDTH):
            acc = acc + ua_ref[pl.ds(r0 + k * batch, CONV_ROW_BLOCK), :] * caw_ref[k:k + 1, :]
        mu = jnp.mean(acc, axis=-1, keepdims=True)
        cen = acc - mu
        var = jnp.mean(cen * cen, axis=-1, keepdims=True)
        y = cen * lax.rsqrt(var + LN_EPS) * lng_ref[...] + lnb_ref[...]
        ycat_ref[pl.ds(r0, CONV_ROW_BLOCK), 0:D_CONV] = (y * jax.nn.sigmoid(y)).astype(_bf16)
        return carry

    lax.fori_loop(0, rows // CONV_ROW_BLOCK, conv_a_block, 0)

    def conv_b_block(j, carry):
        r0 = pl.multiple_of(j * CONV_ROW_BLOCK, CONV_ROW_BLOCK)
        acc = jnp.broadcast_to(cbb_ref[...], (CONV_ROW_BLOCK, D_RNN))
        for k in range(CONV_B_WIDTH):
            acc = acc + xb_ref[pl.ds(r0 + k * batch, CONV_ROW_BLOCK), :] * cbw_ref[k:k + 1, :]
        xc_ref[pl.ds(r0, CONV_ROW_BLOCK), :] = acc
        return carry

    lax.fori_loop(0, rows // CONV_ROW_BLOCK, conv_b_block, 0)

    @pl.when(i == pl.num_programs(0) - 1)
    def _store_conv_state():
        na_ref[...] = ua_ref[rows:rows + ha, :]
        nb_ref[...] = xb_ref[rows:rows + hb, :]

    ua_ref[0:ha, :] = ua_ref[rows:rows + ha, :]
    xb_ref[0:hb, :] = xb_ref[rows:rows + hb, :]

    lam = lam_ref[...]
    neg_c_softplus = -RGLRU_C * (jnp.maximum(-lam, 0.0) + jnp.log1p(jnp.exp(-jnp.abs(lam))))
    if first_chunk:
        row_id = lax.broadcasted_iota(jnp.int32, (rows, GATE_BLOCK), 0)
        is_reset = jnp.logical_and(row_id < batch, i == 0)
    for blk in range(D_RNN // GATE_BLOCK):
        cols = slice(blk * GATE_BLOCK, (blk + 1) * GATE_BLOCK)
        xc = xc_ref[:, cols]
        xc_bf = xc.astype(_bf16)
        r_t = jax.nn.sigmoid(_dot(xc_bf, wr_ref[blk]) + br_ref[:, cols])
        i_t = jax.nn.sigmoid(_dot(xc_bf, wi_ref[blk]) + bi_ref[:, cols])
        a = jnp.exp(r_t * neg_c_softplus[:, cols])
        mult = jnp.sqrt(jnp.maximum(1.0 - a * a, 0.0))
        if first_chunk:
            mult = jnp.where(is_reset, 1.0, mult)
            a = jnp.where(is_reset, 0.0, a)
        a_ref[:, cols] = a
        b_ref[:, cols] = mult * (i_t * xc)

    def scan_step(t, h):
        r0 = pl.multiple_of(t * batch, batch)
        h = a_ref[pl.ds(r0, batch), :] * h + b_ref[pl.ds(r0, batch), :]
        b_ref[pl.ds(r0, batch), :] = h
        return h

    h_last = lax.fori_loop(0, steps, scan_step, h_ref[...], unroll=8)
    h_ref[...] = h_last

    @pl.when(i == pl.num_programs(0) - 1)
    def _store_h():
        hl_ref[...] = h_last

    ycat_ref[:, D_CONV:] = (b_ref[...] * gb_ref[...]).astype(_bf16)

    x1 = x + _dot(ycat_ref[...], wout_ref[...])

    hn2 = _rms_scale(x1, n2g_ref[...]).astype(_bf16)
    for c0, c1 in FFN_CHUNKS:
        g = _dot(hn2, wfi_ref[:, c0:c1])
        v = _dot(hn2, wfi_ref[:, D_FF + c0:D_FF + c1])
        act_ref[:, c0:c1] = (g * jax.nn.sigmoid(g) * v).astype(_bf16)
    x2 = x1 + _dot(act_ref[...], wfo_ref[...])
    if final:
        x2 = _rms_scale(x2, fng_ref[...])
    xo_ref[...] = x2


def _const_spec(shape, single_buffer=False):
    zeros = (0,) * len(shape)
    if single_buffer:
        return pl.BlockSpec(shape, lambda i: zeros, pipeline_mode=pl.Buffered(1))
    return pl.BlockSpec(shape, lambda i: zeros)


def _layer_call(x, bufa, bufb, h0, small, big, *, batch, steps, first_chunk, final):
    total_rows = x.shape[0]
    rows = batch * steps
    assert total_rows % rows == 0 and steps >= HIST_A and rows % CONV_ROW_BLOCK == 0
    n_tiles = total_rows // rows
    (n1g, caw, cab, lng, lnb, cbw, cbb, br, bi, lam, n2g, fng) = small
    (win, wr, wi, wout, wfi, wfo) = big
    ha, hb = HIST_A * batch, HIST_B * batch

    row_spec = pl.BlockSpec((rows, D_MODEL), lambda i: (i, 0))
    in_specs = [
        row_spec, _const_spec(bufa.shape), _const_spec(bufb.shape), _const_spec(h0.shape),
        _const_spec(n1g.shape), _const_spec(win.shape, True), _const_spec(caw.shape),
        _const_spec(cab.shape), _const_spec(lng.shape), _const_spec(lnb.shape),
        _const_spec(cbw.shape), _const_spec(cbb.shape), _const_spec(wr.shape, True),
        _const_spec(br.shape), _const_spec(wi.shape, True), _const_spec(bi.shape),
        _const_spec(lam.shape), _const_spec(wout.shape, True), _const_spec(n2g.shape),
        _const_spec(wfi.shape, True), _const_spec(wfo.shape, True), _const_spec(fng.shape),
    ]
    out_shape = (
        jax.ShapeDtypeStruct((total_rows, D_MODEL), _f32),
        jax.ShapeDtypeStruct((ha, D_CONV), _f32),
        jax.ShapeDtypeStruct((hb, D_RNN), _f32),
        jax.ShapeDtypeStruct((batch, D_RNN), _f32),
    )
    out_specs = (row_spec, _const_spec((ha, D_CONV)), _const_spec((hb, D_RNN)),
                 _const_spec((batch, D_RNN)))
    scratch = [
        pltpu.VMEM((ha + rows, D_CONV), _f32),
        pltpu.VMEM((hb + rows, D_RNN), _f32),
        pltpu.VMEM((rows, D_RNN), _f32),
        pltpu.VMEM((rows, D_RNN), _f32),
        pltpu.VMEM((rows, D_RNN), _f32),
        pltpu.VMEM((rows, D_RNN), _f32),
        pltpu.VMEM((rows, D_MODEL), _bf16),
        pltpu.VMEM((rows, D_FF), _bf16),
        pltpu.VMEM((batch, D_RNN), _f32),
    ]
    kern = functools.partial(_layer_kernel, batch=batch, steps=steps,
                             first_chunk=first_chunk, final=final)
    return pl.pallas_call(
        kern,
        grid=(n_tiles,),
        in_specs=in_specs,
        out_specs=out_specs,
        out_shape=out_shape,
        scratch_shapes=scratch,
        compiler_params=pltpu.CompilerParams(
            dimension_semantics=("arbitrary",), vmem_limit_bytes=VMEM_LIMIT_BYTES),
        name="layer_final" if final else "layer",
    )(x, bufa, bufb, h0, n1g, win, caw, cab, lng, lnb, cbw, cbb, wr, br, wi, bi, lam,
      wout, n2g, wfi, wfo, fng)


def _block_diag_gate(w):
    depth = w.shape[0]
    per = GATE_BLOCK // RNN_HEAD_DIM
    w = w.reshape(depth, N_RNN_HEADS // per, per, RNN_HEAD_DIM, RNN_HEAD_DIM)
    eye = jnp.eye(per, dtype=w.dtype)
    out = jnp.einsum('dgpij,pq->dgpiqj', w, eye)
    return out.reshape(depth, N_RNN_HEADS // per, GATE_BLOCK, GATE_BLOCK).astype(_bf16)


def _trunk(x, bufs_a, bufs_b, h0s, first_chunk, params, steps):
    bsz, s, _ = x.shape
    depth = bufs_a.shape[0]
    xt = jnp.transpose(x, (1, 0, 2)).reshape(s * bsz, D_MODEL)
    ta = jnp.transpose(bufs_a, (0, 2, 1, 3)).reshape(depth, HIST_A * bsz, D_CONV)
    tb = jnp.transpose(bufs_b, (0, 2, 1, 3)).reshape(depth, HIST_B * bsz, D_RNN)
    new_a, new_b, new_h = [], [], []
    for l in range(depth):
        small = tuple(p[l] for p in params['small']) + (params['final_g'],)
        big = tuple(p[l] for p in params['big'])
        xt, na, nb, nh = _layer_call(
            xt, ta[l], tb[l], h0s[l], small, big, batch=bsz, steps=steps,
            first_chunk=first_chunk, final=(l == depth - 1))
        new_a.append(na.reshape(HIST_A, bsz, D_CONV))
        new_b.append(nb.reshape(HIST_B, bsz, D_RNN))
        new_h.append(nh)
    y = jnp.transpose(xt.reshape(s, bsz, D_MODEL), (1, 0, 2))
    sa = jnp.transpose(jnp.stack(new_a), (0, 2, 1, 3))
    sb = jnp.transpose(jnp.stack(new_b), (0, 2, 1, 3))
    return y, sa, sb, jnp.stack(new_h)


def kernel(x_prompt, x_sample, state_conv_a, state_conv_b, state_rglru, norm1_g, w_in, conv_a_w, conv_a_b, ln_a_g, ln_a_b, conv_b_w, conv_b_b, gate_r_w, gate_r_b, gate_i_w, gate_i_b, rglru_lambda, w_out, norm2_g, w_ffn_in, w_ffn_out, final_norm_g):
    depth = w_in.shape[0]
    row = lambda p: p.reshape(depth, 1, p.shape[-1])
    params = {
        'small': (row(norm1_g), conv_a_w, row(conv_a_b), row(ln_a_g), row(ln_a_b), conv_b_w,
                  row(conv_b_b), row(gate_r_b), row(gate_i_b), row(rglru_lambda), row(norm2_g)),
        'final_g': final_norm_g.reshape(1, D_MODEL),
        'big': (w_in.astype(_bf16), _block_diag_gate(gate_r_w), _block_diag_gate(gate_i_w),
                w_out.astype(_bf16), w_ffn_in.astype(_bf16), w_ffn_out.astype(_bf16)),
    }
    b = x_prompt.shape[0]
    dt = x_prompt.dtype
    zero_a = jnp.zeros((depth, b, HIST_A, D_CONV), dt)
    zero_b = jnp.zeros((depth, b, HIST_B, D_RNN), dt)
    zero_h = jnp.zeros((depth, b, D_RNN), dt)
    y_prompt, pa, pb, ph = _trunk(x_prompt, zero_a, zero_b, zero_h, True, params,
                                  steps=ROWS_PER_TILE // b)
    y_sample, sa, sb, sh = _trunk(x_sample, state_conv_a, state_conv_b, state_rglru, False,
                                  params, steps=x_sample.shape[1])
    return (y_prompt, y_sample, pa, pb, ph, sa, sb, sh)
```

```python
import functools

import jax
import jax.numpy as jnp
from jax import lax
from jax.experimental import pallas as pl
from jax.experimental.pallas import tpu as pltpu

D_MODEL = 1024
D_CONV = 512
D_RNN = 512
N_RNN_HEADS = 8
RNN_HEAD_DIM = D_RNN // N_RNN_HEADS
CONV_A_WIDTH = 31
CONV_B_WIDTH = 4
HIST_A = CONV_A_WIDTH - 1
HIST_B = CONV_B_WIDTH - 1
RGLRU_C = 8.0
D_FF = 2816
RMS_EPS = 1e-6
LN_EPS = 1e-5

MXU_TILE_V7X = 256
GATE_BLOCK = MXU_TILE_V7X
SUBLANES = 8
LANES = 128
ROWS_PER_TILE = 512
CONV_WINDOW = 8
FFN_CHUNKS = ((0, 512), (512, 1024), (1024, 1536), (1536, 2048), (2048, 2560), (2560, 2816))
VMEM_LIMIT_BYTES = 56 * 1024 * 1024

_bf16 = jnp.bfloat16
_f32 = jnp.float32


def _dot(a, b):
    return jnp.dot(a, b, preferred_element_type=_f32)


def _rms_scale(x, g):
    return x * lax.rsqrt(jnp.mean(x * x, axis=-1, keepdims=True) + RMS_EPS) * g


def _layer_kernel(x_ref, bufa_ref, bufb_ref, h0_ref, n1g_ref, win_ref, caw_ref, cab_ref,
                  lng_ref, lnb_ref, cbw_ref, cbb_ref, wr_ref, br_ref, wi_ref, bi_ref, lam_ref,
                  wout_ref, n2g_ref, wfi_ref, wfo_ref, fng_ref,
                  xo_ref, na_ref, nb_ref, hl_ref,
                  ua_ref, xb_ref, ca_ref, xc_ref, a_ref, b_ref, gb_ref, ycat_ref, act_ref, h_ref,
                  x1_ref, hn2_ref,
                  *, batch, steps, first_chunk, final):
    i = pl.program_id(0)
    last_tile = pl.num_programs(0) - 2
    rows = batch * steps
    ha = HIST_A * batch
    hb = HIST_B * batch

    @pl.when(i == 0)
    def _load_state():
        ua_ref[0:ha, :] = bufa_ref[...]
        xb_ref[0:hb, :] = bufb_ref[...]
        h_ref[...] = h0_ref[...]
        x1_ref[...] = jnp.zeros_like(x1_ref)
        hn2_ref[...] = jnp.zeros_like(hn2_ref)

    hn = _rms_scale(x_ref[...], n1g_ref[...]).astype(_bf16)
    a_val = _dot(hn, win_ref[:, 0:D_CONV])
    a_gate = _dot(hn, win_ref[:, D_CONV:2 * D_CONV])
    ua_ref[ha:ha + rows, :] = a_val * jax.nn.sigmoid(a_gate)
    xb_ref[hb:hb + rows, :] = _dot(hn, win_ref[:, 2 * D_CONV:2 * D_CONV + D_RNN])
    gb_ref[...] = jax.nn.gelu(_dot(hn, win_ref[:, 2 * D_CONV + D_RNN:]))

    def conv_unit(src_ref, w_ref, bias_ref, width, dst_ref, win, grp, ties):
        t0 = win * CONV_WINDOW
        lanes = slice(grp * LANES, (grp + 1) * LANES)
        accs = [jnp.broadcast_to(bias_ref[:, lanes], (batch, LANES))] * CONV_WINDOW
        for s in range(t0, t0 + CONV_WINDOW + width - 1):
            u = src_ref[s * batch:(s + 1) * batch, lanes]
            if s == t0:
                bits = pltpu.bitcast(ties[0], jnp.uint32)
                for other in ties[1:]:
                    bits = bits | pltpu.bitcast(other, jnp.uint32)
                u = pltpu.bitcast(pltpu.bitcast(u, jnp.uint32) | ((bits >> 16) >> 16), _f32)
            for t in range(CONV_WINDOW):
                k = s - t0 - t
                if 0 <= k < width:
                    accs[t] = accs[t] + u * w_ref[k:k + 1, lanes]
        for t in range(CONV_WINDOW):
            dst_ref[(t0 + t) * batch:(t0 + t + 1) * batch, lanes] = accs[t]
        return accs[-1]

    def conv_a_unit(win, grp, ties):
        return conv_unit(ua_ref, caw_ref, cab_ref, CONV_A_WIDTH, ca_ref, win, grp, ties)

    def conv_b_unit(win, grp, ties):
        return conv_unit(xb_ref, cbw_ref, cbb_ref, CONV_B_WIDTH, xc_ref, win, grp, ties)

    def norm_a_window(win):
        r0, r1 = win * CONV_WINDOW * batch, (win + 1) * CONV_WINDOW * batch
        ca = ca_ref[r0:r1, :]
        mu = jnp.mean(ca, axis=-1, keepdims=True)
        cen = ca - mu
        var = jnp.mean(cen * cen, axis=-1, keepdims=True)
        y = cen * lax.rsqrt(var + LN_EPS) * lng_ref[...] + lnb_ref[...]
        ycat_ref[r0:r1, 0:D_CONV] = (y * jax.nn.sigmoid(y)).astype(_bf16)

    groups = D_CONV // LANES
    n_units = (steps // CONV_WINDOW) * groups
    n_dots = 2 * len(FFN_CHUNKS)
    unit = 0
    chain = []
    for c, (c0, c1) in enumerate(FFN_CHUNKS):
        g = _dot(hn2_ref[...], wfi_ref[:, c0:c1])
        v = _dot(hn2_ref[...], wfi_ref[:, D_FF + c0:D_FF + c1])
        for d, res in ((2 * c, g), (2 * c + 1, v)):
            while unit < n_units and (unit * n_dots) // n_units == d:
                last = conv_a_unit(unit // groups, unit % groups, [res[0:batch, 0:LANES]] + chain)
                chain = [last]
                if unit % groups == groups - 1:
                    norm_a_window(unit // groups)
                unit += 1
        if c == len(FFN_CHUNKS) - 1:
            for ub in range(n_units):
                last = conv_b_unit(ub // groups, ub % groups, [v[0:batch, 0:LANES]] + chain)
                chain = [last]
        act_ref[:, c0:c1] = (g * jax.nn.sigmoid(g) * v).astype(_bf16)
    assert unit == n_units

    lam = lam_ref[...]
    neg_c_softplus = -RGLRU_C * (jnp.maximum(-lam, 0.0) + jnp.log1p(jnp.exp(-jnp.abs(lam))))
    if first_chunk:
        row_id = lax.broadcasted_iota(jnp.int32, (rows, GATE_BLOCK), 0)
        is_reset = jnp.logical_and(row_id < batch, i == 0)
    gate_pre = []
    for blk in range(D_RNN // GATE_BLOCK):
        xc_bf = xc_ref[:, blk * GATE_BLOCK:(blk + 1) * GATE_BLOCK].astype(_bf16)
        gate_pre.append((_dot(xc_bf, wr_ref[blk]), _dot(xc_bf, wi_ref[blk])))

    ffn_out = _dot(act_ref[...], wfo_ref[...])

    for blk in range(D_RNN // GATE_BLOCK):
        cols = slice(blk * GATE_BLOCK, (blk + 1) * GATE_BLOCK)
        r_t = jax.nn.sigmoid(gate_pre[blk][0] + br_ref[:, cols])
        i_t = jax.nn.sigmoid(gate_pre[blk][1] + bi_ref[:, cols])
        a = jnp.exp(r_t * neg_c_softplus[:, cols])
        mult = jnp.sqrt(jnp.maximum(1.0 - a * a, 0.0))
        if first_chunk:
            mult = jnp.where(is_reset, 1.0, mult)
            a = jnp.where(is_reset, 0.0, a)
        a_ref[:, cols] = a
        b_ref[:, cols] = mult * (i_t * xc_ref[:, cols])

    h = h_ref[...]
    for t in range(steps):
        r0 = t * batch
        h = a_ref[r0:r0 + batch, :] * h + b_ref[r0:r0 + batch, :]
        gb_ref[r0:r0 + batch, :] = h * gb_ref[r0:r0 + batch, :]
    h_ref[...] = h
    ycat_ref[:, D_CONV:] = gb_ref[...].astype(_bf16)

    x2 = x1_ref[...] + ffn_out
    if final:
        x2 = _rms_scale(x2, fng_ref[...])
    xo_ref[...] = x2

    x1 = x_ref[...] + _dot(ycat_ref[...], wout_ref[...])
    x1_ref[...] = x1
    hn2_ref[...] = _rms_scale(x1, n2g_ref[...]).astype(_bf16)

    ua_ref[0:ha, :] = ua_ref[rows:rows + ha, :]
    xb_ref[0:hb, :] = xb_ref[rows:rows + hb, :]

    @pl.when(i == last_tile)
    def _store_state():
        na_ref[...] = ua_ref[0:ha, :]
        nb_ref[...] = xb_ref[0:hb, :]
        hl_ref[...] = h_ref[...]


def _const_spec(shape, single_buffer=False):
    zeros = (0,) * len(shape)
    if single_buffer:
        return pl.BlockSpec(shape, lambda i: zeros, pipeline_mode=pl.Buffered(1))
    return pl.BlockSpec(shape, lambda i: zeros)


def _layer_call(x, bufa, bufb, h0, small, big, *, batch, steps, first_chunk, final):
    total_rows = x.shape[0]
    rows = batch * steps
    assert total_rows % rows == 0 and steps >= HIST_A and steps % CONV_WINDOW == 0
    assert batch % SUBLANES == 0
    n_tiles = total_rows // rows
    (n1g, caw, cab, lng, lnb, cbw, cbb, br, bi, lam, n2g, fng) = small
    (win, wr, wi, wout, wfi, wfo) = big
    ha, hb = HIST_A * batch, HIST_B * batch

    in_row_spec = pl.BlockSpec((rows, D_MODEL), lambda i: (jnp.minimum(i, n_tiles - 1), 0))
    out_row_spec = pl.BlockSpec((rows, D_MODEL), lambda i: (jnp.maximum(i - 1, 0), 0))
    in_specs = [
        in_row_spec, _const_spec(bufa.shape), _const_spec(bufb.shape), _const_spec(h0.shape),
        _const_spec(n1g.shape), _const_spec(win.shape, True), _const_spec(caw.shape),
        _const_spec(cab.shape), _const_spec(lng.shape), _const_spec(lnb.shape),
        _const_spec(cbw.shape), _const_spec(cbb.shape), _const_spec(wr.shape, True),
        _const_spec(br.shape), _const_spec(wi.shape, True), _const_spec(bi.shape),
        _const_spec(lam.shape), _const_spec(wout.shape, True), _const_spec(n2g.shape),
        _const_spec(wfi.shape, True), _const_spec(wfo.shape, True), _const_spec(fng.shape),
    ]
    out_shape = (
        jax.ShapeDtypeStruct((total_rows, D_MODEL), _f32),
        jax.ShapeDtypeStruct((ha, D_CONV), _f32),
        jax.ShapeDtypeStruct((hb, D_RNN), _f32),
        jax.ShapeDtypeStruct((batch, D_RNN), _f32),
    )
    out_specs = (out_row_spec, _const_spec((ha, D_CONV)), _const_spec((hb, D_RNN)),
                 _const_spec((batch, D_RNN)))
    scratch = [
        pltpu.VMEM((ha + rows, D_CONV), _f32),
        pltpu.VMEM((hb + rows, D_RNN), _f32),
        pltpu.VMEM((rows, D_CONV), _f32),
        pltpu.VMEM((rows, D_RNN), _f32),
        pltpu.VMEM((rows, D_RNN), _f32),
        pltpu.VMEM((rows, D_RNN), _f32),
        pltpu.VMEM((rows, D_RNN), _f32),
        pltpu.VMEM((rows, D_MODEL), _bf16),
        pltpu.VMEM((rows, D_FF), _bf16),
        pltpu.VMEM((batch, D_RNN), _f32),
        pltpu.VMEM((rows, D_MODEL), _f32),
        pltpu.VMEM((rows, D_MODEL), _bf16),
    ]
    kern = functools.partial(_layer_kernel, batch=batch, steps=steps,
                             first_chunk=first_chunk, final=final)
    return pl.pallas_call(
        kern,
        grid=(n_tiles + 1,),
        in_specs=in_specs,
        out_specs=out_specs,
        out_shape=out_shape,
        scratch_shapes=scratch,
        compiler_params=pltpu.CompilerParams(
            dimension_semantics=("arbitrary",), vmem_limit_bytes=VMEM_LIMIT_BYTES),
        name="layer_final" if final else "layer",
    )(x, bufa, bufb, h0, n1g, win, caw, cab, lng, lnb, cbw, cbb, wr, br, wi, bi, lam,
      wout, n2g, wfi, wfo, fng)


def _block_diag_gate(w):
    depth = w.shape[0]
    per = GATE_BLOCK // RNN_HEAD_DIM
    w = w.reshape(depth, N_RNN_HEADS // per, per, RNN_HEAD_DIM, RNN_HEAD_DIM)
    eye = jnp.eye(per, dtype=w.dtype)
    out = jnp.einsum('dgpij,pq->dgpiqj', w, eye)
    return out.reshape(depth, N_RNN_HEADS // per, GATE_BLOCK, GATE_BLOCK).astype(_bf16)


def _trunk(x, bufs_a, bufs_b, h0s, first_chunk, params, steps):
    bsz, s, _ = x.shape
    depth = bufs_a.shape[0]
    xt = jnp.transpose(x, (1, 0, 2)).reshape(s * bsz, D_MODEL)
    ta = jnp.transpose(bufs_a, (0, 2, 1, 3)).reshape(depth, HIST_A * bsz, D_CONV)
    tb = jnp.transpose(bufs_b, (0, 2, 1, 3)).reshape(depth, HIST_B * bsz, D_RNN)
    new_a, new_b, new_h = [], [], []
    for l in range(depth):
        small = tuple(p[l] for p in params['small']) + (params['final_g'],)
        big = tuple(p[l] for p in params['big'])
        xt, na, nb, nh = _layer_call(
            xt, ta[l], tb[l], h0s[l], small, big, batch=bsz, steps=steps,
            first_chunk=first_chunk, final=(l == depth - 1))
        new_a.append(na.reshape(HIST_A, bsz, D_CONV))
        new_b.append(nb.reshape(HIST_B, bsz, D_RNN))
        new_h.append(nh)
    y = jnp.transpose(xt.reshape(s, bsz, D_MODEL), (1, 0, 2))
    sa = jnp.transpose(jnp.stack(new_a), (0, 2, 1, 3))
    sb = jnp.transpose(jnp.stack(new_b), (0, 2, 1, 3))
    return y, sa, sb, jnp.stack(new_h)


def kernel(x_prompt, x_sample, state_conv_a, state_conv_b, state_rglru, norm1_g, w_in, conv_a_w, conv_a_b, ln_a_g, ln_a_b, conv_b_w, conv_b_b, gate_r_w, gate_r_b, gate_i_w, gate_i_b, rglru_lambda, w_out, norm2_g, w_ffn_in, w_ffn_out, final_norm_g):
    depth = w_in.shape[0]
    row = lambda p: p.reshape(depth, 1, p.shape[-1])
    params = {
        'small': (row(norm1_g), conv_a_w, row(conv_a_b), row(ln_a_g), row(ln_a_b), conv_b_w,
                  row(conv_b_b), row(gate_r_b), row(gate_i_b), row(rglru_lambda), row(norm2_g)),
        'final_g': final_norm_g.reshape(1, D_MODEL),
        'big': (w_in.astype(_bf16), _block_diag_gate(gate_r_w), _block_diag_gate(gate_i_w),
                w_out.astype(_bf16), w_ffn_in.astype(_bf16), w_ffn_out.astype(_bf16)),
    }
    b = x_prompt.shape[0]
    dt = x_prompt.dtype
    zero_a = jnp.zeros((depth, b, HIST_A, D_CONV), dt)
    zero_b = jnp.zeros((depth, b, HIST_B, D_RNN), dt)
    zero_h = jnp.zeros((depth, b, D_RNN), dt)
    y_prompt, pa, pb, ph = _trunk(x_prompt, zero_a, zero_b, zero_h, True, params,
                                  steps=ROWS_PER_TILE // b)
    y_sample, sa, sb, sh = _trunk(x_sample, state_conv_a, state_conv_b, state_rglru, False,
                                  params, steps=x_sample.shape[1])
    return (y_prompt, y_sample, pa, pb, ph, sa, sb, sh)
```

```python
import functools

import jax
import jax.numpy as jnp
from jax import lax
from jax.experimental import pallas as pl
from jax.experimental.pallas import tpu as pltpu

D_MODEL = 1024
D_CONV = 512
D_RNN = 512
N_RNN_HEADS = 8
RNN_HEAD_DIM = D_RNN // N_RNN_HEADS
CONV_A_WIDTH = 31
CONV_B_WIDTH = 4
HIST_A = CONV_A_WIDTH - 1
HIST_B = CONV_B_WIDTH - 1
RGLRU_C = 8.0
D_FF = 2816
RMS_EPS = 1e-6
LN_EPS = 1e-5

MXU_TILE_V7X = 256
GATE_BLOCK = MXU_TILE_V7X
SUBLANES = 8
LANES = 128
ROWS_PER_TILE = 512
CONV_WINDOW = 8
FFN_CHUNKS = ((0, 512), (512, 1024), (1024, 1536), (1536, 2048), (2048, 2560), (2560, 2816))
VMEM_LIMIT_BYTES = 56 * 1024 * 1024
CAST_VMEM_LIMIT_BYTES = 40 * 1024 * 1024

_bf16 = jnp.bfloat16
_f32 = jnp.float32


def _dot(a, b):
    return jnp.dot(a, b, preferred_element_type=_f32)


def _rms_scale(x, g):
    return x * lax.rsqrt(jnp.mean(x * x, axis=-1, keepdims=True) + RMS_EPS) * g


def _layer_kernel(x_ref, bufa_ref, bufb_ref, h0_ref, n1g_ref, win_ref, caw_ref, cab_ref,
                  lng_ref, lnb_ref, cbw_ref, cbb_ref, wr_ref, br_ref, wi_ref, bi_ref, lam_ref,
                  wout_ref, n2g_ref, wfi_ref, wfo_ref, fng_ref,
                  xo_ref, na_ref, nb_ref, hl_ref,
                  ua_ref, xb_ref, ca_ref, xc_ref, a_ref, b_ref, gb_ref, ycat_ref, act_ref, h_ref,
                  x1_ref, hn2_ref, *maybe_xtm_ref,
                  batch, steps, first_chunk, final, batch_major_in, batch_major_out):
    i = pl.program_id(0)
    last_tile = pl.num_programs(0) - 2
    rows = batch * steps
    ha = HIST_A * batch
    hb = HIST_B * batch

    @pl.when(i == 0)
    def _load_state():
        ua_ref[0:ha, :] = bufa_ref[...]
        xb_ref[0:hb, :] = bufb_ref[...]
        h_ref[...] = h0_ref[...]
        x1_ref[...] = jnp.zeros_like(x1_ref)
        hn2_ref[...] = jnp.zeros_like(hn2_ref)

    if batch_major_in:
        (xtm_ref,) = maybe_xtm_ref
        for t in range(steps):
            xtm_ref[t * batch:(t + 1) * batch, :] = x_ref[:, t, :]
        x_ref = xtm_ref
    hn = _rms_scale(x_ref[...], n1g_ref[...]).astype(_bf16)
    a_val = _dot(hn, win_ref[:, 0:D_CONV])
    a_gate = _dot(hn, win_ref[:, D_CONV:2 * D_CONV])
    ua_ref[ha:ha + rows, :] = a_val * jax.nn.sigmoid(a_gate)
    xb_ref[hb:hb + rows, :] = _dot(hn, win_ref[:, 2 * D_CONV:2 * D_CONV + D_RNN])
    gb_ref[...] = jax.nn.gelu(_dot(hn, win_ref[:, 2 * D_CONV + D_RNN:]))

    def conv_unit(src_ref, w_ref, bias_ref, width, dst_ref, win, grp, ties):
        t0 = win * CONV_WINDOW
        lanes = slice(grp * LANES, (grp + 1) * LANES)
        accs = [jnp.broadcast_to(bias_ref[:, lanes], (batch, LANES))] * CONV_WINDOW
        for s in range(t0, t0 + CONV_WINDOW + width - 1):
            u = src_ref[s * batch:(s + 1) * batch, lanes]
            if s == t0:
                bits = pltpu.bitcast(ties[0], jnp.uint32)
                for other in ties[1:]:
                    bits = bits | pltpu.bitcast(other, jnp.uint32)
                u = pltpu.bitcast(pltpu.bitcast(u, jnp.uint32) | ((bits >> 16) >> 16), _f32)
            for t in range(CONV_WINDOW):
                k = s - t0 - t
                if 0 <= k < width:
                    accs[t] = accs[t] + u * w_ref[k:k + 1, lanes]
        for t in range(CONV_WINDOW):
            dst_ref[(t0 + t) * batch:(t0 + t + 1) * batch, lanes] = accs[t]
        return accs[-1]

    def conv_a_unit(win, grp, ties):
        return conv_unit(ua_ref, caw_ref, cab_ref, CONV_A_WIDTH, ca_ref, win, grp, ties)

    def conv_b_unit(win, grp, ties):
        return conv_unit(xb_ref, cbw_ref, cbb_ref, CONV_B_WIDTH, xc_ref, win, grp, ties)

    def norm_a_window(win):
        r0, r1 = win * CONV_WINDOW * batch, (win + 1) * CONV_WINDOW * batch
        ca = ca_ref[r0:r1, :]
        mu = jnp.mean(ca, axis=-1, keepdims=True)
        cen = ca - mu
        var = jnp.mean(cen * cen, axis=-1, keepdims=True)
        y = cen * lax.rsqrt(var + LN_EPS) * lng_ref[...] + lnb_ref[...]
        ycat_ref[r0:r1, 0:D_CONV] = (y * jax.nn.sigmoid(y)).astype(_bf16)

    groups = D_CONV // LANES
    n_units = (steps // CONV_WINDOW) * groups
    n_dots = 2 * len(FFN_CHUNKS)
    unit = 0
    chain = []
    for c, (c0, c1) in enumerate(FFN_CHUNKS):
        g = _dot(hn2_ref[...], wfi_ref[:, c0:c1])
        v = _dot(hn2_ref[...], wfi_ref[:, D_FF + c0:D_FF + c1])
        for d, res in ((2 * c, g), (2 * c + 1, v)):
            while unit < n_units and (unit * n_dots) // n_units == d:
                last = conv_a_unit(unit // groups, unit % groups, [res[0:batch, 0:LANES]] + chain)
                chain = [last]
                if unit % groups == groups - 1:
                    norm_a_window(unit // groups)
                unit += 1
        if c == len(FFN_CHUNKS) - 1:
            for ub in range(n_units):
                last = conv_b_unit(ub // groups, ub % groups, [v[0:batch, 0:LANES]] + chain)
                chain = [last]
        act_ref[:, c0:c1] = (g * jax.nn.sigmoid(g) * v).astype(_bf16)
    assert unit == n_units

    lam = lam_ref[...]
    neg_c_softplus = -RGLRU_C * (jnp.maximum(-lam, 0.0) + jnp.log1p(jnp.exp(-jnp.abs(lam))))
    if first_chunk:
        row_id = lax.broadcasted_iota(jnp.int32, (rows, GATE_BLOCK), 0)
        is_reset = jnp.logical_and(row_id < batch, i == 0)
    gate_pre = []
    for blk in range(D_RNN // GATE_BLOCK):
        xc_bf = xc_ref[:, blk * GATE_BLOCK:(blk + 1) * GATE_BLOCK].astype(_bf16)
        gate_pre.append((_dot(xc_bf, wr_ref[blk]), _dot(xc_bf, wi_ref[blk])))

    ffn_out = _dot(act_ref[...], wfo_ref[...])

    for blk in range(D_RNN // GATE_BLOCK):
        cols = slice(blk * GATE_BLOCK, (blk + 1) * GATE_BLOCK)
        r_t = jax.nn.sigmoid(gate_pre[blk][0] + br_ref[:, cols])
        i_t = jax.nn.sigmoid(gate_pre[blk][1] + bi_ref[:, cols])
        a = jnp.exp(r_t * neg_c_softplus[:, cols])
        mult = jnp.sqrt(jnp.maximum(1.0 - a * a, 0.0))
        if first_chunk:
            mult = jnp.where(is_reset, 1.0, mult)
            a = jnp.where(is_reset, 0.0, a)
        a_ref[:, cols] = a
        b_ref[:, cols] = mult * (i_t * xc_ref[:, cols])

    h = h_ref[...]
    for t in range(steps):
        r0 = t * batch
        h = a_ref[r0:r0 + batch, :] * h + b_ref[r0:r0 + batch, :]
        gb_ref[r0:r0 + batch, :] = h * gb_ref[r0:r0 + batch, :]
    h_ref[...] = h
    ycat_ref[:, D_CONV:] = gb_ref[...].astype(_bf16)

    x2 = x1_ref[...] + ffn_out
    if final:
        x2 = _rms_scale(x2, fng_ref[...])
    if batch_major_out:
        for t in range(steps):
            xo_ref[:, t, :] = x2[t * batch:(t + 1) * batch, :]
    else:
        xo_ref[...] = x2

    x1 = x_ref[...] + _dot(ycat_ref[...], wout_ref[...])
    x1_ref[...] = x1
    hn2_ref[...] = _rms_scale(x1, n2g_ref[...]).astype(_bf16)

    ua_ref[0:ha, :] = ua_ref[rows:rows + ha, :]
    xb_ref[0:hb, :] = xb_ref[rows:rows + hb, :]

    @pl.when(i == last_tile)
    def _store_state():
        na_ref[...] = ua_ref[0:ha, :]
        nb_ref[...] = xb_ref[0:hb, :]
        hl_ref[...] = h_ref[...]


def _layer_spec(arr, layer, single_buffer=False):
    block = (None,) + arr.shape[1:]
    index = (layer,) + (0,) * (arr.ndim - 1)
    if single_buffer:
        return pl.BlockSpec(block, lambda i: index, pipeline_mode=pl.Buffered(1))
    return pl.BlockSpec(block, lambda i: index)


def _whole_spec(shape):
    zeros = (0,) * len(shape)
    return pl.BlockSpec(shape, lambda i: zeros)


def _layer_call(x, bufa, bufb, h0, small, big, fng, *, layer, batch, steps, first_chunk, final,
                batch_major_in, batch_major_out):
    total_steps = x.shape[1] if batch_major_in else x.shape[0] // batch
    rows = batch * steps
    assert total_steps % steps == 0 and steps >= HIST_A and steps % CONV_WINDOW == 0
    assert batch % SUBLANES == 0
    n_tiles = total_steps // steps
    ha, hb = HIST_A * batch, HIST_B * batch

    def tile_spec(batch_major, tile_of_step):
        if batch_major:
            return pl.BlockSpec((batch, steps, D_MODEL), lambda i: (0, tile_of_step(i), 0))
        return pl.BlockSpec((rows, D_MODEL), lambda i: (tile_of_step(i), 0))

    in_specs = [tile_spec(batch_major_in, lambda i: jnp.minimum(i, n_tiles - 1)),
                _layer_spec(bufa, layer), _layer_spec(bufb, layer), _layer_spec(h0, layer)]
    operands = [x, bufa, bufb, h0]
    for name in _KERNEL_PARAM_ORDER:
        arr = big[name] if name in big else small[name]
        in_specs.append(_layer_spec(arr, layer, single_buffer=name in big))
        operands.append(arr)
    in_specs.append(_whole_spec(fng.shape))
    operands.append(fng)

    if batch_major_out:
        x_out_shape = jax.ShapeDtypeStruct((batch, total_steps, D_MODEL), _f32)
    else:
        x_out_shape = jax.ShapeDtypeStruct((total_steps * batch, D_MODEL), _f32)
    out_shape = (
        x_out_shape,
        jax.ShapeDtypeStruct((ha, D_CONV), _f32),
        jax.ShapeDtypeStruct((hb, D_RNN), _f32),
        jax.ShapeDtypeStruct((batch, D_RNN), _f32),
    )
    out_specs = (tile_spec(batch_major_out, lambda i: jnp.maximum(i - 1, 0)),
                 _whole_spec((ha, D_CONV)), _whole_spec((hb, D_RNN)), _whole_spec((batch, D_RNN)))
    scratch = [
        pltpu.VMEM((ha + rows, D_CONV), _f32),
        pltpu.VMEM((hb + rows, D_RNN), _f32),
        pltpu.VMEM((rows, D_CONV), _f32),
        pltpu.VMEM((rows, D_RNN), _f32),
        pltpu.VMEM((rows, D_RNN), _f32),
        pltpu.VMEM((rows, D_RNN), _f32),
        pltpu.VMEM((rows, D_RNN), _f32),
        pltpu.VMEM((rows, D_MODEL), _bf16),
        pltpu.VMEM((rows, D_FF), _bf16),
        pltpu.VMEM((batch, D_RNN), _f32),
        pltpu.VMEM((rows, D_MODEL), _f32),
        pltpu.VMEM((rows, D_MODEL), _bf16),
    ]
    if batch_major_in:
        scratch.append(pltpu.VMEM((rows, D_MODEL), _f32))
    kern = functools.partial(_layer_kernel, batch=batch, steps=steps, first_chunk=first_chunk,
                             final=final, batch_major_in=batch_major_in,
                             batch_major_out=batch_major_out)
    return pl.pallas_call(
        kern,
        grid=(n_tiles + 1,),
        in_specs=in_specs,
        out_specs=out_specs,
        out_shape=out_shape,
        scratch_shapes=scratch,
        compiler_params=pltpu.CompilerParams(
            dimension_semantics=("arbitrary",), vmem_limit_bytes=VMEM_LIMIT_BYTES),
        name="layer_%d" % layer,
    )(*operands)


_KERNEL_PARAM_ORDER = ('n1g', 'win', 'caw', 'cab', 'lng', 'lnb', 'cbw', 'cbb', 'wr', 'br', 'wi', 'bi',
                       'lam', 'wout', 'n2g', 'wfi', 'wfo')


def _cast_kernel(w_ref, o_ref):
    o_ref[...] = w_ref[...].astype(o_ref.dtype)


def _to_bf16(w, block_rows):
    depth, k, n = w.shape
    assert k % block_rows == 0
    spec = pl.BlockSpec((None, block_rows, n), lambda d, r: (d, r, 0))
    return pl.pallas_call(
        _cast_kernel,
        grid=(depth, k // block_rows),
        in_specs=[spec],
        out_specs=spec,
        out_shape=jax.ShapeDtypeStruct(w.shape, _bf16),
        compiler_params=pltpu.CompilerParams(
            dimension_semantics=("arbitrary", "arbitrary"), vmem_limit_bytes=CAST_VMEM_LIMIT_BYTES),
        name="cast_bf16",
    )(w)


def _block_diag_gate(w):
    depth = w.shape[0]
    per = GATE_BLOCK // RNN_HEAD_DIM
    w = w.reshape(depth, N_RNN_HEADS // per, per, RNN_HEAD_DIM, RNN_HEAD_DIM)
    eye = jnp.eye(per, dtype=w.dtype)
    out = jnp.einsum('dgpij,pq->dgpiqj', w, eye)
    return out.reshape(depth, N_RNN_HEADS // per, GATE_BLOCK, GATE_BLOCK).astype(_bf16)


def _trunk(x, bufs_a, bufs_b, h0s, first_chunk, small, big, fng, steps):
    bsz = x.shape[0]
    depth = bufs_a.shape[0]
    ta = jnp.transpose(bufs_a, (0, 2, 1, 3)).reshape(depth, HIST_A * bsz, D_CONV)
    tb = jnp.transpose(bufs_b, (0, 2, 1, 3)).reshape(depth, HIST_B * bsz, D_RNN)
    new_a, new_b, new_h = [], [], []
    for l in range(depth):
        x, na, nb, nh = _layer_call(
            x, ta, tb, h0s, small, big, fng, layer=l, batch=bsz, steps=steps,
            first_chunk=first_chunk, final=(l == depth - 1),
            batch_major_in=(l == 0), batch_major_out=(l == depth - 1))
        new_a.append(na.reshape(HIST_A, bsz, D_CONV))
        new_b.append(nb.reshape(HIST_B, bsz, D_RNN))
        new_h.append(nh)
    sa = jnp.transpose(jnp.stack(new_a), (0, 2, 1, 3))
    sb = jnp.transpose(jnp.stack(new_b), (0, 2, 1, 3))
    return x, sa, sb, jnp.stack(new_h)


def kernel(x_prompt, x_sample, state_conv_a, state_conv_b, state_rglru, norm1_g, w_in, conv_a_w, conv_a_b, ln_a_g, ln_a_b, conv_b_w, conv_b_b, gate_r_w, gate_r_b, gate_i_w, gate_i_b, rglru_lambda, w_out, norm2_g, w_ffn_in, w_ffn_out, final_norm_g):
    depth = w_in.shape[0]
    row = lambda p: p.reshape(depth, 1, p.shape[-1])
    small = {'n1g': row(norm1_g), 'caw': conv_a_w, 'cab': row(conv_a_b), 'lng': row(ln_a_g),
             'lnb': row(ln_a_b), 'cbw': conv_b_w, 'cbb': row(conv_b_b), 'br': row(gate_r_b),
             'bi': row(gate_i_b), 'lam': row(rglru_lambda), 'n2g': row(norm2_g)}
    big = {'win': _to_bf16(w_in, 512), 'wr': _block_diag_gate(gate_r_w),
           'wi': _block_diag_gate(gate_i_w), 'wout': _to_bf16(w_out, 1024),
           'wfi': _to_bf16(w_ffn_in, 256), 'wfo': _to_bf16(w_ffn_out, D_FF // 2)}
    fng = final_norm_g.reshape(1, D_MODEL)
    b = x_prompt.shape[0]
    dt = x_prompt.dtype
    zero_a = jnp.zeros((depth, b, HIST_A, D_CONV), dt)
    zero_b = jnp.zeros((depth, b, HIST_B, D_RNN), dt)
    zero_h = jnp.zeros((depth, b, D_RNN), dt)
    y_prompt, pa, pb, ph = _trunk(x_prompt, zero_a, zero_b, zero_h, True, small, big, fng,
                                  steps=ROWS_PER_TILE // b)
    y_sample, sa, sb, sh = _trunk(x_sample, state_conv_a, state_conv_b, state_rglru, False,
                                  small, big, fng, steps=x_sample.shape[1])
    return (y_prompt, y_sample, pa, pb, ph, sa, sb, sh)
```

```python
import functools

import jax
import jax.numpy as jnp
from jax import lax
from jax.experimental import pallas as pl
from jax.experimental.pallas import tpu as pltpu

D_MODEL = 1024
D_CONV = 512
D_RNN = 512
N_RNN_HEADS = 8
RNN_HEAD_DIM = D_RNN // N_RNN_HEADS
CONV_A_WIDTH = 31
CONV_B_WIDTH = 4
HIST_A = CONV_A_WIDTH - 1
HIST_B = CONV_B_WIDTH - 1
RGLRU_C = 8.0
D_FF = 2816
RMS_EPS = 1e-6
LN_EPS = 1e-5

MXU_TILE_V7X = 256
GATE_BLOCK = MXU_TILE_V7X
SUBLANES = 8
LANES = 128
ROWS_PER_TILE = 512
CONV_WINDOW = 8
FFN_CHUNKS = ((0, 512), (512, 1024), (1024, 1536), (1536, 2048), (2048, 2560), (2560, 2816))
VMEM_LIMIT_BYTES = 56 * 1024 * 1024
CAST_VMEM_LIMIT_BYTES = 40 * 1024 * 1024

_bf16 = jnp.bfloat16
_f32 = jnp.float32


def _dot(a, b):
    return jnp.dot(a, b, preferred_element_type=_f32)


def _rms_scale(x, g):
    return x * lax.rsqrt(jnp.mean(x * x, axis=-1, keepdims=True) + RMS_EPS) * g


def _layer_kernel(x_ref, bufa_ref, bufb_ref, h0_ref, n1g_ref, win_ref, caw_ref, cab_ref,
                  lng_ref, lnb_ref, cbw_ref, cbb_ref, wr_ref, br_ref, wi_ref, bi_ref, lam_ref,
                  wout_ref, n2g_ref, wfi_ref, wfo_ref, fng_ref,
                  xo_ref, na_ref, nb_ref, hl_ref,
                  ua_ref, xb_ref, ca_ref, xc_ref, a_ref, b_ref, gb_ref, ycat_ref, act_ref, h_ref,
                  x1_ref, hn2_ref, *maybe_xtm_ref,
                  batch, steps, first_chunk, final, batch_major_in, batch_major_out, skew):
    i = pl.program_id(0)
    last_tile = pl.num_programs(0) - (2 if skew else 1)
    rows = batch * steps
    ha = HIST_A * batch
    hb = HIST_B * batch

    @pl.when(i == 0)
    def _load_state():
        ua_ref[0:ha, :] = bufa_ref[...]
        xb_ref[0:hb, :] = bufb_ref[...]
        h_ref[...] = h0_ref[...]
        x1_ref[...] = jnp.zeros_like(x1_ref)
        hn2_ref[...] = jnp.zeros_like(hn2_ref)

    if batch_major_in:
        (xtm_ref,) = maybe_xtm_ref
        for t in range(steps):
            xtm_ref[t * batch:(t + 1) * batch, :] = x_ref[:, t, :]
        x_ref = xtm_ref

    def mixer_in():
        hn = _rms_scale(x_ref[...], n1g_ref[...]).astype(_bf16)
        a_val = _dot(hn, win_ref[:, 0:D_CONV])
        a_gate = _dot(hn, win_ref[:, D_CONV:2 * D_CONV])
        ua_ref[ha:ha + rows, :] = a_val * jax.nn.sigmoid(a_gate)
        xb_ref[hb:hb + rows, :] = _dot(hn, win_ref[:, 2 * D_CONV:2 * D_CONV + D_RNN])
        gb_ref[...] = jax.nn.gelu(_dot(hn, win_ref[:, 2 * D_CONV + D_RNN:]))

    def conv_unit(src_ref, w_ref, bias_ref, width, dst_ref, win, grp, ties):
        t0 = win * CONV_WINDOW
        lanes = slice(grp * LANES, (grp + 1) * LANES)
        accs = [jnp.broadcast_to(bias_ref[:, lanes], (batch, LANES))] * CONV_WINDOW
        for s in range(t0, t0 + CONV_WINDOW + width - 1):
            u = src_ref[s * batch:(s + 1) * batch, lanes]
            if s == t0 and ties:
                bits = pltpu.bitcast(ties[0], jnp.uint32)
                for other in ties[1:]:
                    bits = bits | pltpu.bitcast(other, jnp.uint32)
                u = pltpu.bitcast(pltpu.bitcast(u, jnp.uint32) | ((bits >> 16) >> 16), _f32)
            for t in range(CONV_WINDOW):
                k = s - t0 - t
                if 0 <= k < width:
                    accs[t] = accs[t] + u * w_ref[k:k + 1, lanes]
        for t in range(CONV_WINDOW):
            dst_ref[(t0 + t) * batch:(t0 + t + 1) * batch, lanes] = accs[t]
        return accs[-1]

    def norm_a_window(win):
        r0, r1 = win * CONV_WINDOW * batch, (win + 1) * CONV_WINDOW * batch
        ca = ca_ref[r0:r1, :]
        mu = jnp.mean(ca, axis=-1, keepdims=True)
        cen = ca - mu
        var = jnp.mean(cen * cen, axis=-1, keepdims=True)
        y = cen * lax.rsqrt(var + LN_EPS) * lng_ref[...] + lnb_ref[...]
        ycat_ref[r0:r1, 0:D_CONV] = (y * jax.nn.sigmoid(y)).astype(_bf16)

    def convs(tie_points):
        groups = D_CONV // LANES
        n_units = (steps // CONV_WINDOW) * groups
        chain = []
        for unit in range(n_units):
            ties = list(chain)
            if tie_points:
                ties.append(tie_points[(unit * len(tie_points)) // n_units][0:batch, 0:LANES])
            chain = [conv_unit(ua_ref, caw_ref, cab_ref, CONV_A_WIDTH, ca_ref,
                               unit // groups, unit % groups, ties)]
            if unit % groups == groups - 1:
                norm_a_window(unit // groups)
        for unit in range(n_units):
            ties = list(chain)
            if tie_points:
                ties.append(tie_points[-1][0:batch, 0:LANES])
            chain = [conv_unit(xb_ref, cbw_ref, cbb_ref, CONV_B_WIDTH, xc_ref,
                               unit // groups, unit % groups, ties)]

    def ffn_in():
        dots = []
        for c0, c1 in FFN_CHUNKS:
            dots.append(_dot(hn2_ref[...], wfi_ref[:, c0:c1]))
            dots.append(_dot(hn2_ref[...], wfi_ref[:, D_FF + c0:D_FF + c1]))
        return dots

    def ffn_act(dots):
        for c, (c0, c1) in enumerate(FFN_CHUNKS):
            g, v = dots[2 * c], dots[2 * c + 1]
            act_ref[:, c0:c1] = (g * jax.nn.sigmoid(g) * v).astype(_bf16)

    def ffn_out():
        x2 = x1_ref[...] + _dot(act_ref[...], wfo_ref[...])
        if final:
            x2 = _rms_scale(x2, fng_ref[...])
        if batch_major_out:
            for t in range(steps):
                xo_ref[:, t, :] = x2[t * batch:(t + 1) * batch, :]
        else:
            xo_ref[...] = x2

    def rglru():
        lam = lam_ref[...]
        neg_c_softplus = -RGLRU_C * (jnp.maximum(-lam, 0.0) + jnp.log1p(jnp.exp(-jnp.abs(lam))))
        if first_chunk:
            row_id = lax.broadcasted_iota(jnp.int32, (rows, GATE_BLOCK), 0)
            is_reset = jnp.logical_and(row_id < batch, i == 0)
        for blk in range(D_RNN // GATE_BLOCK):
            cols = slice(blk * GATE_BLOCK, (blk + 1) * GATE_BLOCK)
            xc_bf = xc_ref[:, cols].astype(_bf16)
            r_t = jax.nn.sigmoid(_dot(xc_bf, wr_ref[blk]) + br_ref[:, cols])
            i_t = jax.nn.sigmoid(_dot(xc_bf, wi_ref[blk]) + bi_ref[:, cols])
            a = jnp.exp(r_t * neg_c_softplus[:, cols])
            mult = jnp.sqrt(jnp.maximum(1.0 - a * a, 0.0))
            if first_chunk:
                mult = jnp.where(is_reset, 1.0, mult)
                a = jnp.where(is_reset, 0.0, a)
            a_ref[:, cols] = a
            b_ref[:, cols] = mult * (i_t * xc_ref[:, cols])
        h = h_ref[...]
        for t in range(steps):
            r0 = t * batch
            h = a_ref[r0:r0 + batch, :] * h + b_ref[r0:r0 + batch, :]
            gb_ref[r0:r0 + batch, :] = h * gb_ref[r0:r0 + batch, :]
        h_ref[...] = h
        ycat_ref[:, D_CONV:] = gb_ref[...].astype(_bf16)

    def mixer_out():
        x1 = x_ref[...] + _dot(ycat_ref[...], wout_ref[...])
        x1_ref[...] = x1
        hn2_ref[...] = _rms_scale(x1, n2g_ref[...]).astype(_bf16)

    mixer_in()
    if skew:
        dots = ffn_in()
        convs(dots)
        ffn_act(dots)
        rglru()
        ffn_out()
        mixer_out()
    else:
        convs(None)
        rglru()
        mixer_out()
        ffn_act(ffn_in())
        ffn_out()

    ua_ref[0:ha, :] = ua_ref[rows:rows + ha, :]
    xb_ref[0:hb, :] = xb_ref[rows:rows + hb, :]

    @pl.when(i == last_tile)
    def _store_state():
        na_ref[...] = ua_ref[0:ha, :]
        nb_ref[...] = xb_ref[0:hb, :]
        hl_ref[...] = h_ref[...]


def _layer_spec(arr, layer, single_buffer=False):
    block = (None,) + arr.shape[1:]
    index = (layer,) + (0,) * (arr.ndim - 1)
    if single_buffer:
        return pl.BlockSpec(block, lambda i: index, pipeline_mode=pl.Buffered(1))
    return pl.BlockSpec(block, lambda i: index)


def _whole_spec(shape):
    zeros = (0,) * len(shape)
    return pl.BlockSpec(shape, lambda i: zeros)


def _layer_call(x, bufa, bufb, h0, small, big, fng, *, layer, batch, steps, first_chunk, final,
                batch_major_in, batch_major_out):
    total_steps = x.shape[1] if batch_major_in else x.shape[0] // batch
    rows = batch * steps
    assert total_steps % steps == 0 and steps >= HIST_A and steps % CONV_WINDOW == 0
    assert batch % SUBLANES == 0
    n_tiles = total_steps // steps
    skew = n_tiles > 1
    ha, hb = HIST_A * batch, HIST_B * batch

    def tile_spec(batch_major, tile_of_step):
        if batch_major:
            return pl.BlockSpec((batch, steps, D_MODEL), lambda i: (0, tile_of_step(i), 0))
        return pl.BlockSpec((rows, D_MODEL), lambda i: (tile_of_step(i), 0))

    in_tile = (lambda i: jnp.minimum(i, n_tiles - 1)) if skew else (lambda i: i)
    out_tile = (lambda i: jnp.maximum(i - 1, 0)) if skew else (lambda i: i)
    in_specs = [tile_spec(batch_major_in, in_tile),
                _layer_spec(bufa, layer), _layer_spec(bufb, layer), _layer_spec(h0, layer)]
    operands = [x, bufa, bufb, h0]
    for name in _KERNEL_PARAM_ORDER:
        arr = big[name] if name in big else small[name]
        in_specs.append(_layer_spec(arr, layer, single_buffer=name in big))
        operands.append(arr)
    in_specs.append(_whole_spec(fng.shape))
    operands.append(fng)

    if batch_major_out:
        x_out_shape = jax.ShapeDtypeStruct((batch, total_steps, D_MODEL), _f32)
    else:
        x_out_shape = jax.ShapeDtypeStruct((total_steps * batch, D_MODEL), _f32)
    out_shape = (
        x_out_shape,
        jax.ShapeDtypeStruct((ha, D_CONV), _f32),
        jax.ShapeDtypeStruct((hb, D_RNN), _f32),
        jax.ShapeDtypeStruct((batch, D_RNN), _f32),
    )
    out_specs = (tile_spec(batch_major_out, out_tile),
                 _whole_spec((ha, D_CONV)), _whole_spec((hb, D_RNN)), _whole_spec((batch, D_RNN)))
    scratch = [
        pltpu.VMEM((ha + rows, D_CONV), _f32),
        pltpu.VMEM((hb + rows, D_RNN), _f32),
        pltpu.VMEM((rows, D_CONV), _f32),
        pltpu.VMEM((rows, D_RNN), _f32),
        pltpu.VMEM((rows, D_RNN), _f32),
        pltpu.VMEM((rows, D_RNN), _f32),
        pltpu.VMEM((rows, D_RNN), _f32),
        pltpu.VMEM((rows, D_MODEL), _bf16),
        pltpu.VMEM((rows, D_FF), _bf16),
        pltpu.VMEM((batch, D_RNN), _f32),
        pltpu.VMEM((rows, D_MODEL), _f32),
        pltpu.VMEM((rows, D_MODEL), _bf16),
    ]
    if batch_major_in:
        scratch.append(pltpu.VMEM((rows, D_MODEL), _f32))
    kern = functools.partial(_layer_kernel, batch=batch, steps=steps, first_chunk=first_chunk,
                             final=final, batch_major_in=batch_major_in,
                             batch_major_out=batch_major_out, skew=skew)
    return pl.pallas_call(
        kern,
        grid=(n_tiles + 1 if skew else n_tiles,),
        in_specs=in_specs,
        out_specs=out_specs,
        out_shape=out_shape,
        scratch_shapes=scratch,
        compiler_params=pltpu.CompilerParams(
            dimension_semantics=("arbitrary",), vmem_limit_bytes=VMEM_LIMIT_BYTES),
        name="layer_%d" % layer,
    )(*operands)


_KERNEL_PARAM_ORDER = ('n1g', 'win', 'caw', 'cab', 'lng', 'lnb', 'cbw', 'cbb', 'wr', 'br', 'wi', 'bi',
                       'lam', 'wout', 'n2g', 'wfi', 'wfo')


def _cast_kernel(w_ref, o_ref):
    o_ref[...] = w_ref[...].astype(o_ref.dtype)


def _to_bf16(w, block_rows):
    depth, k, n = w.shape
    assert k % block_rows == 0
    spec = pl.BlockSpec((None, block_rows, n), lambda d, r: (d, r, 0))
    return pl.pallas_call(
        _cast_kernel,
        grid=(depth, k // block_rows),
        in_specs=[spec],
        out_specs=spec,
        out_shape=jax.ShapeDtypeStruct(w.shape, _bf16),
        compiler_params=pltpu.CompilerParams(
            dimension_semantics=("arbitrary", "arbitrary"), vmem_limit_bytes=CAST_VMEM_LIMIT_BYTES),
        name="cast_bf16",
    )(w)


def _block_diag_gate(w):
    depth = w.shape[0]
    per = GATE_BLOCK // RNN_HEAD_DIM
    w = w.reshape(depth, N_RNN_HEADS // per, per, RNN_HEAD_DIM, RNN_HEAD_DIM)
    eye = jnp.eye(per, dtype=w.dtype)
    out = jnp.einsum('dgpij,pq->dgpiqj', w, eye)
    return out.reshape(depth, N_RNN_HEADS // per, GATE_BLOCK, GATE_BLOCK).astype(_bf16)


def _trunk(x, bufs_a, bufs_b, h0s, first_chunk, small, big, fng, steps):
    bsz = x.shape[0]
    depth = bufs_a.shape[0]
    ta = jnp.transpose(bufs_a, (0, 2, 1, 3)).reshape(depth, HIST_A * bsz, D_CONV)
    tb = jnp.transpose(bufs_b, (0, 2, 1, 3)).reshape(depth, HIST_B * bsz, D_RNN)
    new_a, new_b, new_h = [], [], []
    for l in range(depth):
        x, na, nb, nh = _layer_call(
            x, ta, tb, h0s, small, big, fng, layer=l, batch=bsz, steps=steps,
            first_chunk=first_chunk, final=(l == depth - 1),
            batch_major_in=(l == 0), batch_major_out=(l == depth - 1))
        new_a.append(na.reshape(HIST_A, bsz, D_CONV))
        new_b.append(nb.reshape(HIST_B, bsz, D_RNN))
        new_h.append(nh)
    sa = jnp.transpose(jnp.stack(new_a), (0, 2, 1, 3))
    sb = jnp.transpose(jnp.stack(new_b), (0, 2, 1, 3))
    return x, sa, sb, jnp.stack(new_h)


def kernel(x_prompt, x_sample, state_conv_a, state_conv_b, state_rglru, norm1_g, w_in, conv_a_w, conv_a_b, ln_a_g, ln_a_b, conv_b_w, conv_b_b, gate_r_w, gate_r_b, gate_i_w, gate_i_b, rglru_lambda, w_out, norm2_g, w_ffn_in, w_ffn_out, final_norm_g):
    depth = w_in.shape[0]
    row = lambda p: p.reshape(depth, 1, p.shape[-1])
    small = {'n1g': row(norm1_g), 'caw': conv_a_w, 'cab': row(conv_a_b), 'lng': row(ln_a_g),
             'lnb': row(ln_a_b), 'cbw': conv_b_w, 'cbb': row(conv_b_b), 'br': row(gate_r_b),
             'bi': row(gate_i_b), 'lam': row(rglru_lambda), 'n2g': row(norm2_g)}
    big = {'win': _to_bf16(w_in, 512), 'wr': _block_diag_gate(gate_r_w),
           'wi': _block_diag_gate(gate_i_w), 'wout': _to_bf16(w_out, 1024),
           'wfi': _to_bf16(w_ffn_in, 256), 'wfo': _to_bf16(w_ffn_out, D_FF // 2)}
    fng = final_norm_g.reshape(1, D_MODEL)
    b = x_prompt.shape[0]
    dt = x_prompt.dtype
    zero_a = jnp.zeros((depth, b, HIST_A, D_CONV), dt)
    zero_b = jnp.zeros((depth, b, HIST_B, D_RNN), dt)
    zero_h = jnp.zeros((depth, b, D_RNN), dt)
    y_prompt, pa, pb, ph = _trunk(x_prompt, zero_a, zero_b, zero_h, True, small, big, fng,
                                  steps=ROWS_PER_TILE // b)
    y_sample, sa, sb, sh = _trunk(x_sample, state_conv_a, state_conv_b, state_rglru, False,
                                  small, big, fng, steps=x_sample.shape[1])
    return (y_prompt, y_sample, pa, pb, ph, sa, sb, sh)
```

```python
import functools

import jax
import jax.numpy as jnp
from jax import lax
from jax.experimental import pallas as pl
from jax.experimental.pallas import tpu as pltpu

D_MODEL = 1024
D_CONV = 512
D_RNN = 512
N_RNN_HEADS = 8
RNN_HEAD_DIM = D_RNN // N_RNN_HEADS
CONV_A_WIDTH = 31
CONV_B_WIDTH = 4
HIST_A = CONV_A_WIDTH - 1
HIST_B = CONV_B_WIDTH - 1
RGLRU_C = 8.0
D_FF = 2816
RMS_EPS = 1e-6
LN_EPS = 1e-5

MXU_TILE_V7X = 256
GATE_BLOCK = MXU_TILE_V7X
SUBLANES = 8
LANES = 128
ROWS_PER_TILE = 512
CONV_WINDOW = 8
TAP_TIE_EVERY = 4
FFN_CHUNKS = ((0, 512), (512, 1024), (1024, 1536), (1536, 2048), (2048, 2560), (2560, 2816))
VMEM_LIMIT_BYTES = 56 * 1024 * 1024
CAST_VMEM_LIMIT_BYTES = 40 * 1024 * 1024

_bf16 = jnp.bfloat16
_f32 = jnp.float32


def _dot(a, b):
    return jnp.dot(a, b, preferred_element_type=_f32)


def _rms_scale(x, g):
    return x * lax.rsqrt(jnp.mean(x * x, axis=-1, keepdims=True) + RMS_EPS) * g


def _layer_kernel(x_ref, bufa_ref, bufb_ref, h0_ref, n1g_ref, win_ref, caw_ref, cab_ref,
                  lng_ref, lnb_ref, cbw_ref, cbb_ref, wr_ref, br_ref, wi_ref, bi_ref, lam_ref,
                  wout_ref, n2g_ref, wfi_ref, wfo_ref, fng_ref,
                  xo_ref, na_ref, nb_ref, hl_ref,
                  ua_ref, xb_ref, ca_ref, xc_ref, a_ref, b_ref, gb_ref, ycat_ref, act_ref, h_ref,
                  x1_ref, hn2_ref, *maybe_xtm_ref,
                  batch, steps, first_chunk, final, batch_major_in, batch_major_out, skew):
    i = pl.program_id(0)
    last_tile = pl.num_programs(0) - (2 if skew else 1)
    rows = batch * steps
    ha = HIST_A * batch
    hb = HIST_B * batch

    @pl.when(i == 0)
    def _load_state():
        ua_ref[0:ha, :] = bufa_ref[...]
        xb_ref[0:hb, :] = bufb_ref[...]
        h_ref[...] = h0_ref[...]
        x1_ref[...] = jnp.zeros_like(x1_ref)
        hn2_ref[...] = jnp.zeros_like(hn2_ref)

    if batch_major_in:
        (xtm_ref,) = maybe_xtm_ref
        for t in range(steps):
            xtm_ref[t * batch:(t + 1) * batch, :] = x_ref[:, t, :]
        x_ref = xtm_ref

    def mixer_in():
        hn = _rms_scale(x_ref[...], n1g_ref[...]).astype(_bf16)
        a_val = _dot(hn, win_ref[:, 0:D_CONV])
        a_gate = _dot(hn, win_ref[:, D_CONV:2 * D_CONV])
        ua_ref[ha:ha + rows, :] = a_val * jax.nn.sigmoid(a_gate)
        xb_ref[hb:hb + rows, :] = _dot(hn, win_ref[:, 2 * D_CONV:2 * D_CONV + D_RNN])
        gb_ref[...] = jax.nn.gelu(_dot(hn, win_ref[:, 2 * D_CONV + D_RNN:]))

    def conv_unit(src_ref, w_ref, bias_ref, width, dst_ref, win, grp, ties):
        t0 = win * CONV_WINDOW
        lanes = slice(grp * LANES, (grp + 1) * LANES)
        accs = [jnp.broadcast_to(bias_ref[:, lanes], (batch, LANES))] * CONV_WINDOW
        for k in range(width):
            tap = jnp.broadcast_to(w_ref[k:k + 1, lanes], (SUBLANES, LANES))
            if k > 0 and k % TAP_TIE_EVERY == 0:
                zero = (pltpu.bitcast(accs[-1][0:SUBLANES], jnp.uint32) >> 16) >> 16
                tap = pltpu.bitcast(pltpu.bitcast(tap, jnp.uint32) | zero, _f32)
            tap = jnp.concatenate([tap] * (batch // SUBLANES), axis=0)
            for t in range(CONV_WINDOW):
                s = t0 + t + k
                u = src_ref[s * batch:(s + 1) * batch, lanes]
                if k == 0 and t == 0 and ties:
                    bits = pltpu.bitcast(ties[0], jnp.uint32)
                    for other in ties[1:]:
                        bits = bits | pltpu.bitcast(other, jnp.uint32)
                    u = pltpu.bitcast(pltpu.bitcast(u, jnp.uint32) | ((bits >> 16) >> 16), _f32)
                accs[t] = accs[t] + u * tap
        for t in range(CONV_WINDOW):
            dst_ref[(t0 + t) * batch:(t0 + t + 1) * batch, lanes] = accs[t]
        return accs[-1]

    def norm_a_window(win):
        r0, r1 = win * CONV_WINDOW * batch, (win + 1) * CONV_WINDOW * batch
        ca = ca_ref[r0:r1, :]
        mu = jnp.mean(ca, axis=-1, keepdims=True)
        cen = ca - mu
        var = jnp.mean(cen * cen, axis=-1, keepdims=True)
        y = cen * lax.rsqrt(var + LN_EPS) * lng_ref[...] + lnb_ref[...]
        ycat_ref[r0:r1, 0:D_CONV] = (y * jax.nn.sigmoid(y)).astype(_bf16)

    def convs(tie_points):
        groups = D_CONV // LANES
        n_units = (steps // CONV_WINDOW) * groups
        chain = []
        for unit in range(n_units):
            ties = list(chain)
            if tie_points:
                ties.append(tie_points[(unit * len(tie_points)) // n_units][0:batch, 0:LANES])
            chain = [conv_unit(ua_ref, caw_ref, cab_ref, CONV_A_WIDTH, ca_ref,
                               unit // groups, unit % groups, ties)]
            if unit % groups == groups - 1:
                norm_a_window(unit // groups)
        for unit in range(n_units):
            ties = list(chain)
            if tie_points:
                ties.append(tie_points[-1][0:batch, 0:LANES])
            chain = [conv_unit(xb_ref, cbw_ref, cbb_ref, CONV_B_WIDTH, xc_ref,
                               unit // groups, unit % groups, ties)]

    def ffn_in():
        dots = []
        for c0, c1 in FFN_CHUNKS:
            dots.append(_dot(hn2_ref[...], wfi_ref[:, c0:c1]))
            dots.append(_dot(hn2_ref[...], wfi_ref[:, D_FF + c0:D_FF + c1]))
        return dots

    def ffn_act(dots):
        for c, (c0, c1) in enumerate(FFN_CHUNKS):
            g, v = dots[2 * c], dots[2 * c + 1]
            act_ref[:, c0:c1] = (g * jax.nn.sigmoid(g) * v).astype(_bf16)

    def ffn_out():
        x2 = x1_ref[...] + _dot(act_ref[...], wfo_ref[...])
        if final:
            x2 = _rms_scale(x2, fng_ref[...])
        if batch_major_out:
            for t in range(steps):
                xo_ref[:, t, :] = x2[t * batch:(t + 1) * batch, :]
        else:
            xo_ref[...] = x2

    def rglru():
        lam = lam_ref[...]
        neg_c_softplus = -RGLRU_C * (jnp.maximum(-lam, 0.0) + jnp.log1p(jnp.exp(-jnp.abs(lam))))
        if first_chunk:
            row_id = lax.broadcasted_iota(jnp.int32, (rows, GATE_BLOCK), 0)
            is_reset = jnp.logical_and(row_id < batch, i == 0)
        for blk in range(D_RNN // GATE_BLOCK):
            cols = slice(blk * GATE_BLOCK, (blk + 1) * GATE_BLOCK)
            xc_bf = xc_ref[:, cols].astype(_bf16)
            r_t = jax.nn.sigmoid(_dot(xc_bf, wr_ref[blk]) + br_ref[:, cols])
            i_t = jax.nn.sigmoid(_dot(xc_bf, wi_ref[blk]) + bi_ref[:, cols])
            a = jnp.exp(r_t * neg_c_softplus[:, cols])
            mult = jnp.sqrt(jnp.maximum(1.0 - a * a, 0.0))
            if first_chunk:
                mult = jnp.where(is_reset, 1.0, mult)
                a = jnp.where(is_reset, 0.0, a)
            a_ref[:, cols] = a
            b_ref[:, cols] = mult * (i_t * xc_ref[:, cols])
        h = h_ref[...]
        for t in range(steps):
            r0 = t * batch
            h = a_ref[r0:r0 + batch, :] * h + b_ref[r0:r0 + batch, :]
            gb_ref[r0:r0 + batch, :] = h * gb_ref[r0:r0 + batch, :]
        h_ref[...] = h
        ycat_ref[:, D_CONV:] = gb_ref[...].astype(_bf16)

    def mixer_out():
        for r0 in range(0, rows, rows // 2):
            rs = slice(r0, r0 + rows // 2)
            x1 = x_ref[rs, :] + _dot(ycat_ref[rs, :], wout_ref[...])
            x1_ref[rs, :] = x1
            hn2_ref[rs, :] = _rms_scale(x1, n2g_ref[...]).astype(_bf16)

    mixer_in()
    if skew:
        dots = ffn_in()
        convs(dots)
        ffn_act(dots)
        rglru()
        ffn_out()
        mixer_out()
    else:
        convs(None)
        rglru()
        mixer_out()
        ffn_act(ffn_in())
        ffn_out()

    ua_ref[0:ha, :] = ua_ref[rows:rows + ha, :]
    xb_ref[0:hb, :] = xb_ref[rows:rows + hb, :]

    @pl.when(i == last_tile)
    def _store_state():
        na_ref[...] = ua_ref[0:ha, :]
        nb_ref[...] = xb_ref[0:hb, :]
        hl_ref[...] = h_ref[...]


def _layer_spec(arr, layer, single_buffer=False):
    block = (None,) + arr.shape[1:]
    index = (layer,) + (0,) * (arr.ndim - 1)
    if single_buffer:
        return pl.BlockSpec(block, lambda i: index, pipeline_mode=pl.Buffered(1))
    return pl.BlockSpec(block, lambda i: index)


def _whole_spec(shape):
    zeros = (0,) * len(shape)
    return pl.BlockSpec(shape, lambda i: zeros)


def _layer_call(x, bufa, bufb, h0, small, big, fng, *, layer, batch, steps, first_chunk, final,
                batch_major_in, batch_major_out):
    total_steps = x.shape[1] if batch_major_in else x.shape[0] // batch
    rows = batch * steps
    assert total_steps % steps == 0 and steps >= HIST_A and steps % CONV_WINDOW == 0
    assert batch % SUBLANES == 0
    n_tiles = total_steps // steps
    skew = n_tiles > 1
    ha, hb = HIST_A * batch, HIST_B * batch

    def tile_spec(batch_major, tile_of_step):
        if batch_major:
            return pl.BlockSpec((batch, steps, D_MODEL), lambda i: (0, tile_of_step(i), 0))
        return pl.BlockSpec((rows, D_MODEL), lambda i: (tile_of_step(i), 0))

    in_tile = (lambda i: jnp.minimum(i, n_tiles - 1)) if skew else (lambda i: i)
    out_tile = (lambda i: jnp.maximum(i - 1, 0)) if skew else (lambda i: i)
    in_specs = [tile_spec(batch_major_in, in_tile),
                _layer_spec(bufa, layer), _layer_spec(bufb, layer), _layer_spec(h0, layer)]
    operands = [x, bufa, bufb, h0]
    for name in _KERNEL_PARAM_ORDER:
        arr = big[name] if name in big else small[name]
        in_specs.append(_layer_spec(arr, layer, single_buffer=name in big))
        operands.append(arr)
    in_specs.append(_whole_spec(fng.shape))
    operands.append(fng)

    if batch_major_out:
        x_out_shape = jax.ShapeDtypeStruct((batch, total_steps, D_MODEL), _f32)
    else:
        x_out_shape = jax.ShapeDtypeStruct((total_steps * batch, D_MODEL), _f32)
    out_shape = (
        x_out_shape,
        jax.ShapeDtypeStruct((ha, D_CONV), _f32),
        jax.ShapeDtypeStruct((hb, D_RNN), _f32),
        jax.ShapeDtypeStruct((batch, D_RNN), _f32),
    )
    out_specs = (tile_spec(batch_major_out, out_tile),
                 _whole_spec((ha, D_CONV)), _whole_spec((hb, D_RNN)), _whole_spec((batch, D_RNN)))
    scratch = [
        pltpu.VMEM((ha + rows, D_CONV), _f32),
        pltpu.VMEM((hb + rows, D_RNN), _f32),
        pltpu.VMEM((rows, D_CONV), _f32),
        pltpu.VMEM((rows, D_RNN), _f32),
        pltpu.VMEM((rows, D_RNN), _f32),
        pltpu.VMEM((rows, D_RNN), _f32),
        pltpu.VMEM((rows, D_RNN), _f32),
        pltpu.VMEM((rows, D_MODEL), _bf16),
        pltpu.VMEM((rows, D_FF), _bf16),
        pltpu.VMEM((batch, D_RNN), _f32),
        pltpu.VMEM((rows, D_MODEL), _f32),
        pltpu.VMEM((rows, D_MODEL), _bf16),
    ]
    if batch_major_in:
        scratch.append(pltpu.VMEM((rows, D_MODEL), _f32))
    kern = functools.partial(_layer_kernel, batch=batch, steps=steps, first_chunk=first_chunk,
                             final=final, batch_major_in=batch_major_in,
                             batch_major_out=batch_major_out, skew=skew)
    return pl.pallas_call(
        kern,
        grid=(n_tiles + 1 if skew else n_tiles,),
        in_specs=in_specs,
        out_specs=out_specs,
        out_shape=out_shape,
        scratch_shapes=scratch,
        compiler_params=pltpu.CompilerParams(
            dimension_semantics=("arbitrary",), vmem_limit_bytes=VMEM_LIMIT_BYTES),
        name="layer_%d" % layer,
    )(*operands)


_KERNEL_PARAM_ORDER = ('n1g', 'win', 'caw', 'cab', 'lng', 'lnb', 'cbw', 'cbb', 'wr', 'br', 'wi', 'bi',
                       'lam', 'wout', 'n2g', 'wfi', 'wfo')


def _cast_kernel(w_ref, o_ref):
    o_ref[...] = w_ref[...].astype(o_ref.dtype)


def _to_bf16(w, block_rows):
    depth, k, n = w.shape
    assert k % block_rows == 0
    spec = pl.BlockSpec((None, block_rows, n), lambda d, r: (d, r, 0))
    return pl.pallas_call(
        _cast_kernel,
        grid=(depth, k // block_rows),
        in_specs=[spec],
        out_specs=spec,
        out_shape=jax.ShapeDtypeStruct(w.shape, _bf16),
        compiler_params=pltpu.CompilerParams(
            dimension_semantics=("arbitrary", "arbitrary"), vmem_limit_bytes=CAST_VMEM_LIMIT_BYTES),
        name="cast_bf16",
    )(w)


def _block_diag_gate(w):
    depth = w.shape[0]
    per = GATE_BLOCK // RNN_HEAD_DIM
    w = w.reshape(depth, N_RNN_HEADS // per, per, RNN_HEAD_DIM, RNN_HEAD_DIM)
    eye = jnp.eye(per, dtype=w.dtype)
    out = jnp.einsum('dgpij,pq->dgpiqj', w, eye)
    return out.reshape(depth, N_RNN_HEADS // per, GATE_BLOCK, GATE_BLOCK).astype(_bf16)


def _trunk(x, bufs_a, bufs_b, h0s, first_chunk, small, big, fng, steps):
    bsz = x.shape[0]
    depth = bufs_a.shape[0]
    ta = jnp.transpose(bufs_a, (0, 2, 1, 3)).reshape(depth, HIST_A * bsz, D_CONV)
    tb = jnp.transpose(bufs_b, (0, 2, 1, 3)).reshape(depth, HIST_B * bsz, D_RNN)
    new_a, new_b, new_h = [], [], []
    for l in range(depth):
        x, na, nb, nh = _layer_call(
            x, ta, tb, h0s, small, big, fng, layer=l, batch=bsz, steps=steps,
            first_chunk=first_chunk, final=(l == depth - 1),
            batch_major_in=(l == 0), batch_major_out=(l == depth - 1))
        new_a.append(na.reshape(HIST_A, bsz, D_CONV))
        new_b.append(nb.reshape(HIST_B, bsz, D_RNN))
        new_h.append(nh)
    sa = jnp.transpose(jnp.stack(new_a), (0, 2, 1, 3))
    sb = jnp.transpose(jnp.stack(new_b), (0, 2, 1, 3))
    return x, sa, sb, jnp.stack(new_h)


def kernel(x_prompt, x_sample, state_conv_a, state_conv_b, state_rglru, norm1_g, w_in, conv_a_w, conv_a_b, ln_a_g, ln_a_b, conv_b_w, conv_b_b, gate_r_w, gate_r_b, gate_i_w, gate_i_b, rglru_lambda, w_out, norm2_g, w_ffn_in, w_ffn_out, final_norm_g):
    depth = w_in.shape[0]
    row = lambda p: p.reshape(depth, 1, p.shape[-1])
    small = {'n1g': row(norm1_g), 'caw': conv_a_w, 'cab': row(conv_a_b), 'lng': row(ln_a_g),
             'lnb': row(ln_a_b), 'cbw': conv_b_w, 'cbb': row(conv_b_b), 'br': row(gate_r_b),
             'bi': row(gate_i_b), 'lam': row(rglru_lambda), 'n2g': row(norm2_g)}
    big = {'win': _to_bf16(w_in, 512), 'wr': _block_diag_gate(gate_r_w),
           'wi': _block_diag_gate(gate_i_w), 'wout': _to_bf16(w_out, 1024),
           'wfi': _to_bf16(w_ffn_in, 256), 'wfo': _to_bf16(w_ffn_out, D_FF // 2)}
    fng = final_norm_g.reshape(1, D_MODEL)
    b = x_prompt.shape[0]
    dt = x_prompt.dtype
    zero_a = jnp.zeros((depth, b, HIST_A, D_CONV), dt)
    zero_b = jnp.zeros((depth, b, HIST_B, D_RNN), dt)
    zero_h = jnp.zeros((depth, b, D_RNN), dt)
    y_prompt, pa, pb, ph = _trunk(x_prompt, zero_a, zero_b, zero_h, True, small, big, fng,
                                  steps=ROWS_PER_TILE // b)
    y_sample, sa, sb, sh = _trunk(x_sample, state_conv_a, state_conv_b, state_rglru, False,
                                  small, big, fng, steps=x_sample.shape[1])
    return (y_prompt, y_sample, pa, pb, ph, sa, sb, sh)
```
